```python
import jax, jax.numpy as jnp
from jax import lax
import numpy as np

D_MODEL = 1024
BATCH = 16
SEQ = 4096
DEPTH = 1
DEC_BATCH = 16
DEC_SEQ = 16
PAST_LEN = 1024

CHUNK = 64
N_HEADS = 16
N_KV_HEADS = 4
HEAD_DIM = 64
Q_GROUP = N_HEADS // N_KV_HEADS
WINDOW = 128
WINDOW_CHUNKS = WINDOW // CHUNK
ATTN_W = N_HEADS * HEAD_DIM
KV_W = N_KV_HEADS * HEAD_DIM
GMLP_CHUNK = 128
GMLP_GROUPS = 4
GMLP_W = D_MODEL
GMLP_GW = GMLP_W // GMLP_GROUPS
D_FF = ((8 * D_MODEL // 3 + 255) // 256) * 256
IN_W = ATTN_W + 2 * KV_W + 2 * GMLP_W + 2 * D_MODEL
EPS = 1e-6
NEG_INF = -1e30

kernel_name = "chunk_causal_swa_sink_gmlp_hybrid_step"


def rms_norm(x, g):
    xf = x.astype(jnp.float32)
    y = xf * lax.rsqrt(jnp.mean(xf * xf, axis=-1, keepdims=True) + EPS)
    return (y * g.astype(jnp.float32)).astype(x.dtype)


def layer_norm(x, g, b):
    xf = x.astype(jnp.float32)
    mu = jnp.mean(xf, axis=-1, keepdims=True)
    xc = xf - mu
    y = xc * lax.rsqrt(jnp.mean(xc * xc, axis=-1, keepdims=True) + EPS)
    return (y * g.astype(jnp.float32) + b.astype(jnp.float32)).astype(x.dtype)


def sink_attention(q, k, v, sinks, mask):
    s = jnp.einsum('...qhgd,...khd->...hgqk', q, k).astype(jnp.float32) * (HEAD_DIM ** -0.5)
    if mask is not None:
        s = jnp.where(mask, s, NEG_INF)
    sk = sinks.astype(jnp.float32)[:, :, None, None]
    m = jnp.maximum(jnp.max(s, axis=-1, keepdims=True), sk)
    p = jnp.exp(s - m)
    denom = jnp.sum(p, axis=-1, keepdims=True) + jnp.exp(sk - m)
    p = (p / denom).astype(v.dtype)
    return jnp.einsum('...hgqk,...khd->...qhgd', p, v)


def attn_prompt(q, k, v, sinks):
    B, S = q.shape[:2]
    nc = S // CHUNK
    pad = WINDOW_CHUNKS * CHUNK
    qc = q.reshape(B, nc, CHUNK, N_KV_HEADS, Q_GROUP, HEAD_DIM)
    kp = jnp.pad(k, ((0, 0), (pad, 0), (0, 0), (0, 0))).reshape(B, nc + WINDOW_CHUNKS, CHUNK, N_KV_HEADS, HEAD_DIM)
    vp = jnp.pad(v, ((0, 0), (pad, 0), (0, 0), (0, 0))).reshape(B, nc + WINDOW_CHUNKS, CHUNK, N_KV_HEADS, HEAD_DIM)
    kb = jnp.concatenate([kp[:, i:i + nc] for i in range(WINDOW_CHUNKS + 1)], axis=2)
    vb = jnp.concatenate([vp[:, i:i + nc] for i in range(WINDOW_CHUNKS + 1)], axis=2)
    kpos = jnp.arange(nc)[:, None] * CHUNK - pad + jnp.arange((WINDOW_CHUNKS + 1) * CHUNK)[None, :]
    mask = (kpos >= 0)[:, None, None, None, :]
    o = sink_attention(qc, kb, vb, sinks, mask)
    return o.reshape(B, S, ATTN_W)


def attn_sample(q, k, v, cache_k, cache_v, sinks):
    B, T = q.shape[:2]
    kc = jnp.concatenate([cache_k, k], axis=1)
    vc = jnp.concatenate([cache_v, v], axis=1)
    o = sink_attention(q.reshape(B, T, N_KV_HEADS, Q_GROUP, HEAD_DIM), kc, vc, sinks, None)
    w = cache_k.shape[1]
    return o.reshape(B, T, ATTN_W), kc[:, -w:], vc[:, -w:]


def spatial_gate(u, v, w_s, b_s):
    L = v.shape[-3]
    wm = jnp.tril(w_s[:, :L, :L])
    mix = jnp.einsum('gts,...sgc->...tgc', wm, v) + b_s[:, :L].T[:, :, None]
    return u * mix


def trunk_layer(x, n_pre_mix, n_post_mix, n_pre_ffn, n_post_ffn, w_in, sinks, ln_g, ln_b, w_s, b_s,
                w_attn_proj, w_gmlp_proj, w_out, w_ffn_in, w_ffn_out, cache_k, cache_v):
    B, T, _ = x.shape
    h = rms_norm(x, n_pre_mix)
    z = h @ w_in
    offs = np.cumsum([ATTN_W, KV_W, KV_W, 2 * GMLP_W, D_MODEL])
    q, k, v, zg, ga, gb = jnp.split(z, offs, axis=-1)
    k = k.reshape(B, T, N_KV_HEADS, HEAD_DIM)
    v = v.reshape(B, T, N_KV_HEADS, HEAD_DIM)
    if cache_k is None:
        a = attn_prompt(q, k, v, sinks)
        new_k, new_v = k[:, -WINDOW:], v[:, -WINDOW:]
    else:
        a, new_k, new_v = attn_sample(q, k, v, cache_k, cache_v, sinks)
    u, gv = jnp.split(jax.nn.gelu(zg), 2, axis=-1)
    gv = layer_norm(gv, ln_g, ln_b)
    if cache_k is None:
        shp = (B, T // GMLP_CHUNK, GMLP_CHUNK, GMLP_GROUPS, GMLP_GW)
    else:
        shp = (B, T, GMLP_GROUPS, GMLP_GW)
    bm = spatial_gate(u.reshape(shp), gv.reshape(shp), w_s, b_s).reshape(B, T, GMLP_W)
    merged = jax.nn.sigmoid(ga) * (a @ w_attn_proj) + jax.nn.sigmoid(gb) * (bm @ w_gmlp_proj)
    x = x + rms_norm(merged @ w_out, n_post_mix)
    h2 = rms_norm(x, n_pre_ffn)
    g, up = jnp.split(h2 @ w_ffn_in, 2, axis=-1)
    x = x + rms_norm((jax.nn.silu(g) * up) @ w_ffn_out, n_post_ffn)
    return x, new_k, new_v, gv


def setup_inputs(seed: int = 0) -> dict:
    key = jax.random.key(seed)
    ks = jax.random.split(key, 20)
    cache_w = min(WINDOW, PAST_LEN)
    nrm = lambda k, shp, s: jax.random.normal(k, shp, jnp.float32) * s
    return {
        "x_prompt": nrm(ks[0], (BATCH, SEQ, D_MODEL), 1.0),
        "x_sample": nrm(ks[1], (DEC_BATCH, DEC_SEQ, D_MODEL), 1.0),
        "cache_attn_k": nrm(ks[2], (DEPTH, DEC_BATCH, cache_w, N_KV_HEADS, HEAD_DIM), 1.0),
        "cache_attn_v": nrm(ks[3], (DEPTH, DEC_BATCH, cache_w, N_KV_HEADS, HEAD_DIM), 1.0),
        "norm_pre_mix": 1.0 + nrm(ks[4], (DEPTH, D_MODEL), 0.05),
        "norm_post_mix": 1.0 + nrm(ks[5], (DEPTH, D_MODEL), 0.05),
        "norm_pre_ffn": 1.0 + nrm(ks[6], (DEPTH, D_MODEL), 0.05),
        "norm_post_ffn": 1.0 + nrm(ks[7], (DEPTH, D_MODEL), 0.05),
        "w_in": nrm(ks[8], (DEPTH, D_MODEL, IN_W), D_MODEL ** -0.5),
        "attn_sinks": nrm(ks[9], (DEPTH, N_KV_HEADS, Q_GROUP), 1.0),
        "gmlp_ln_g": 1.0 + nrm(ks[10], (DEPTH, GMLP_W), 0.05),
        "gmlp_ln_b": nrm(ks[11], (DEPTH, GMLP_W), 0.02),
        "gmlp_w_s": nrm(ks[12], (DEPTH, GMLP_GROUPS, GMLP_CHUNK, GMLP_CHUNK), GMLP_CHUNK ** -0.5),
        "gmlp_b_s": 1.0 + nrm(ks[13], (DEPTH, GMLP_GROUPS, GMLP_CHUNK), 0.1),
        "w_attn_proj": nrm(ks[14], (DEPTH, ATTN_W, D_MODEL), ATTN_W ** -0.5),
        "w_gmlp_proj": nrm(ks[15], (DEPTH, GMLP_W, D_MODEL), GMLP_W ** -0.5),
        "w_out": nrm(ks[16], (DEPTH, D_MODEL, D_MODEL), D_MODEL ** -0.5),
        "w_ffn_in": nrm(ks[17], (DEPTH, D_MODEL, 2 * D_FF), D_MODEL ** -0.5),
        "w_ffn_out": nrm(ks[18], (DEPTH, D_FF, D_MODEL), D_FF ** -0.5),
    }


def reference(x_prompt, x_sample, cache_attn_k, cache_attn_v, norm_pre_mix, norm_post_mix, norm_pre_ffn,
              norm_post_ffn, w_in, attn_sinks, gmlp_ln_g, gmlp_ln_b, gmlp_w_s, gmlp_b_s, w_attn_proj,
              w_gmlp_proj, w_out, w_ffn_in, w_ffn_out):
    xp, xs = x_prompt, x_sample
    kp_l, vp_l, ks_l, vs_l, gs_l = [], [], [], [], []
    for l in range(DEPTH):
        w = (norm_pre_mix[l], norm_post_mix[l], norm_pre_ffn[l], norm_post_ffn[l], w_in[l], attn_sinks[l],
             gmlp_ln_g[l], gmlp_ln_b[l], gmlp_w_s[l], gmlp_b_s[l], w_attn_proj[l], w_gmlp_proj[l], w_out[l],
             w_ffn_in[l], w_ffn_out[l])
        xp, kp, vp, _ = trunk_layer(xp, *w, None, None)
        xs, ks, vs, gvs = trunk_layer(xs, *w, cache_attn_k[l], cache_attn_v[l])
        kp_l.append(kp); vp_l.append(vp); ks_l.append(ks); vs_l.append(vs); gs_l.append(gvs)
    return (xp, xs, jnp.stack(kp_l), jnp.stack(vp_l), jnp.stack(ks_l), jnp.stack(vs_l), jnp.stack(gs_l))
```

```python
import functools

import jax
import jax.numpy as jnp
from jax import lax
from jax.experimental import pallas as pl
from jax.experimental.pallas import tpu as pltpu

D_MODEL = 1024
CHUNK = 64
N_HEADS = 16
N_KV_HEADS = 4
HEAD_DIM = 64
Q_GROUP = N_HEADS // N_KV_HEADS
WINDOW = 128
ATTN_W = N_HEADS * HEAD_DIM
KV_W = N_KV_HEADS * HEAD_DIM
GMLP_CHUNK = 128
GMLP_GROUPS = 4
GMLP_W = D_MODEL
GMLP_GW = GMLP_W // GMLP_GROUPS
D_FF = 2816
IN_W = ATTN_W + 2 * KV_W + 2 * GMLP_W + 2 * D_MODEL
EPS = 1e-6
NEG_INF = -1e30

OFF_K = ATTN_W
OFF_V = OFF_K + KV_W
OFF_U = OFF_V + KV_W
OFF_GV = OFF_U + GMLP_W
OFF_GA = OFF_GV + GMLP_W
OFF_GB = OFF_GA + D_MODEL

KEYS = WINDOW + CHUNK

MIX_TILE = 256
FFN_TILE = 256
VMEM_LIMIT_BYTES = 56 * 1024 * 1024

F32 = jnp.float32
BF16 = jnp.bfloat16


def _dot(a, b):
    return jnp.dot(a, b, preferred_element_type=F32)


def _dot_t(a, b):
    return lax.dot_general(a, b, (((1,), (1,)), ((), ())), preferred_element_type=F32)


def _rms(x, g):
    ms = jnp.mean(x * x, axis=-1, keepdims=True)
    return x * lax.rsqrt(ms + EPS) * g


def _layer_norm(x, g, b):
    mu = jnp.mean(x, axis=-1, keepdims=True)
    xc = x - mu
    var = jnp.mean(xc * xc, axis=-1, keepdims=True)
    return xc * lax.rsqrt(var + EPS) * g + b


def _resident(shape):
    nd = len(shape)
    return pl.BlockSpec(shape, lambda *_: (0,) * nd, pipeline_mode=pl.Buffered(1))


def _mixer_prompt_kernel(x_ref, npre_ref, npost_ref, win_ref, sinks_ref, lng_ref, lnb_ref,
                         ws_ref, bst_ref, wap_ref, wgp_ref, wout_ref,
                         x1_ref, kout_ref, vout_ref,
                         q_s, kext_s, vext_s, a_s, bm_s):
    t = pl.program_id(1)
    nt = pl.num_programs(1)
    T = MIX_TILE

    @pl.when(t == 0)
    def _():
        kext_s[0:WINDOW, :] = jnp.zeros((WINDOW, KV_W), BF16)
        vext_s[0:WINDOW, :] = jnp.zeros((WINDOW, KV_W), BF16)

    x = x_ref[...]
    h = _rms(x, npre_ref[...]).astype(BF16)

    qkv = _dot(h, win_ref[:, 0:OFF_U])
    q_s[...] = (qkv[:, 0:ATTN_W] * (HEAD_DIM ** -0.5)).astype(BF16)
    k = qkv[:, OFF_K:OFF_V]
    v = qkv[:, OFF_V:OFF_U]
    kext_s[WINDOW:WINDOW + T, :] = k.astype(BF16)
    vext_s[WINDOW:WINDOW + T, :] = v.astype(BF16)

    @pl.when(t == nt - 1)
    def _():
        kout_ref[...] = k[T - WINDOW:T, :]
        vout_ref[...] = v[T - WINDOW:T, :]

    for c in range(T // CHUNK):
        r0 = c * CHUNK
        if r0 < WINDOW:
            first_valid = jnp.where(t == 0, WINDOW - r0, 0)
            valid = lax.broadcasted_iota(jnp.int32, (CHUNK, KEYS), 1) >= first_valid
        else:
            valid = None
        for j in range(N_KV_HEADS):
            kj = kext_s[r0:r0 + KEYS, j * HEAD_DIM:(j + 1) * HEAD_DIM]
            vj = vext_s[r0:r0 + KEYS, j * HEAD_DIM:(j + 1) * HEAD_DIM]
            for g in range(Q_GROUP):
                hd = j * Q_GROUP + g
                qh = q_s[r0:r0 + CHUNK, hd * HEAD_DIM:(hd + 1) * HEAD_DIM]
                s = _dot_t(qh, kj)
                if valid is not None:
                    s = jnp.where(valid, s, NEG_INF)
                sk = sinks_ref[j, g]
                m = jnp.maximum(jnp.max(s, axis=-1, keepdims=True), sk)
                p = jnp.exp(s - m)
                denom = jnp.sum(p, axis=-1, keepdims=True) + jnp.exp(sk - m)
                o = _dot(p.astype(BF16), vj) / denom
                a_s[r0:r0 + CHUNK, hd * HEAD_DIM:(hd + 1) * HEAD_DIM] = o.astype(BF16)

    kext_s[0:WINDOW, :] = kext_s[T:T + WINDOW, :]
    vext_s[0:WINDOW, :] = vext_s[T:T + WINDOW, :]

    zg = _dot(h, win_ref[:, OFF_U:OFF_GA])
    u = jax.nn.gelu(zg[:, 0:GMLP_W])
    gv = _layer_norm(jax.nn.gelu(zg[:, GMLP_W:2 * GMLP_W]), lng_ref[...], lnb_ref[...])
    gvb = gv.astype(BF16)
    row = lax.broadcasted_iota(jnp.int32, (GMLP_CHUNK, GMLP_CHUNK), 0)
    col = lax.broadcasted_iota(jnp.int32, (GMLP_CHUNK, GMLP_CHUNK), 1)
    causal = row >= col
    for g in range(GMLP_GROUPS):
        wm = jnp.where(causal, ws_ref[g], 0.0).astype(BF16)
        bias = bst_ref[:, g:g + 1]
        c0 = g * GMLP_GW
        for cc in range(T // GMLP_CHUNK):
            r0 = cc * GMLP_CHUNK
            mix = _dot(wm, gvb[r0:r0 + GMLP_CHUNK, c0:c0 + GMLP_GW]) + bias
            bm_s[r0:r0 + GMLP_CHUNK, c0:c0 + GMLP_GW] = (
                u[r0:r0 + GMLP_CHUNK, c0:c0 + GMLP_GW] * mix).astype(BF16)

    gates = _dot(h, win_ref[:, OFF_GA:IN_W])
    merged = (jax.nn.sigmoid(gates[:, 0:D_MODEL]) * _dot(a_s[...], wap_ref[...])
              + jax.nn.sigmoid(gates[:, D_MODEL:2 * D_MODEL]) * _dot(bm_s[...], wgp_ref[...]))
    mo = _dot(merged.astype(BF16), wout_ref[...])
    x1_ref[...] = x + _rms(mo, npost_ref[...])


def _mixer_prompt(x, npre, npost, win, sinks, lng, lnb, ws, bst, wap, wgp, wout):
    B, S, _ = x.shape
    T = MIX_TILE
    assert S % T == 0 and T % GMLP_CHUNK == 0 and T >= WINDOW
    row_spec = pl.BlockSpec((None, T, D_MODEL), lambda b, t: (b, t, 0))
    kv_spec = pl.BlockSpec((None, WINDOW, KV_W), lambda b, t: (b, 0, 0))
    return pl.pallas_call(
        _mixer_prompt_kernel,
        grid=(B, S // T),
        in_specs=[
            row_spec,
            _resident((1, D_MODEL)), _resident((1, D_MODEL)),
            _resident((D_MODEL, IN_W)),
            pl.BlockSpec(memory_space=pltpu.SMEM),
            _resident((1, GMLP_W)), _resident((1, GMLP_W)),
            _resident((GMLP_GROUPS, GMLP_CHUNK, GMLP_CHUNK)),
            _resident((GMLP_CHUNK, GMLP_GROUPS)),
            _resident((ATTN_W, D_MODEL)), _resident((GMLP_W, D_MODEL)),
            _resident((D_MODEL, D_MODEL)),
        ],
        out_specs=[row_spec, kv_spec, kv_spec],
        out_shape=[
            jax.ShapeDtypeStruct((B, S, D_MODEL), F32),
            jax.ShapeDtypeStruct((B, WINDOW, KV_W), F32),
            jax.ShapeDtypeStruct((B, WINDOW, KV_W), F32),
        ],
        scratch_shapes=[
            pltpu.VMEM((T, ATTN_W), BF16),
            pltpu.VMEM((WINDOW + T, KV_W), BF16),
            pltpu.VMEM((WINDOW + T, KV_W), BF16),
            pltpu.VMEM((T, ATTN_W), BF16),
            pltpu.VMEM((T, GMLP_W), BF16),
        ],
        compiler_params=pltpu.CompilerParams(
            dimension_semantics=("arbitrary", "arbitrary"),
            vmem_limit_bytes=VMEM_LIMIT_BYTES),
        name="mixer_prompt",
    )(x, npre, npost, win, sinks, lng, lnb, ws, bst, wap, wgp, wout)


def _mixer_sample_kernel(x_ref, ck_ref, cv_ref, npre_ref, npost_ref, win_ref, sinks_ref,
                         lng_ref, lnb_ref, wst_ref, bst_ref, wap_ref, wgp_ref, wout_ref,
                         x1_ref, kout_ref, vout_ref, gv_ref,
                         q_s, k_s, v_s, a_s):
    NB, L = ck_ref.shape[0], x_ref.shape[0] // ck_ref.shape[0]
    R = NB * L
    x = x_ref[...]
    h = _rms(x, npre_ref[...]).astype(BF16)

    qkv = _dot(h, win_ref[:, 0:OFF_U])
    q_s[...] = (qkv[:, 0:ATTN_W] * (HEAD_DIM ** -0.5)).astype(BF16)
    k_s[...] = qkv[:, OFF_K:OFF_V]
    v_s[...] = qkv[:, OFF_V:OFF_U]

    def per_stream(b, carry):
        r0 = pl.multiple_of(b * L, L)
        knew = k_s[pl.ds(r0, L), :]
        vnew = v_s[pl.ds(r0, L), :]
        kc = ck_ref[b]
        vc = cv_ref[b]
        kout_ref[b, 0:WINDOW - L, :] = kc[L:WINDOW, :]
        kout_ref[b, WINDOW - L:WINDOW, :] = knew
        vout_ref[b, 0:WINDOW - L, :] = vc[L:WINDOW, :]
        vout_ref[b, WINDOW - L:WINDOW, :] = vnew
        kcb, vcb = kc.astype(BF16), vc.astype(BF16)
        knb, vnb = knew.astype(BF16), vnew.astype(BF16)
        qb = q_s[pl.ds(r0, L), :]
        outs = []
        for j in range(N_KV_HEADS):
            cs = slice(j * HEAD_DIM, (j + 1) * HEAD_DIM)
            for g in range(Q_GROUP):
                hd = j * Q_GROUP + g
                qh = qb[:, hd * HEAD_DIM:(hd + 1) * HEAD_DIM]
                s1 = _dot_t(qh, kcb[:, cs])
                s2 = _dot_t(qh, knb[:, cs])
                sk = sinks_ref[j, g]
                m = jnp.maximum(jnp.maximum(jnp.max(s1, axis=-1, keepdims=True),
                                            jnp.max(s2, axis=-1, keepdims=True)), sk)
                p1 = jnp.exp(s1 - m)
                p2 = jnp.exp(s2 - m)
                denom = (jnp.sum(p1, axis=-1, keepdims=True)
                         + jnp.sum(p2, axis=-1, keepdims=True) + jnp.exp(sk - m))
                o = (_dot(p1.astype(BF16), vcb[:, cs]) + _dot(p2.astype(BF16), vnb[:, cs])) / denom
                outs.append(o)
        a_s[pl.ds(r0, L), :] = jnp.concatenate(outs, axis=-1).astype(BF16)
        return carry

    lax.fori_loop(0, NB, per_stream, 0)

    zg = _dot(h, win_ref[:, OFF_U:OFF_GA])
    u = jax.nn.gelu(zg[:, 0:GMLP_W])
    gv = _layer_norm(jax.nn.gelu(zg[:, GMLP_W:2 * GMLP_W]), lng_ref[...], lnb_ref[...])
    gv_ref[...] = gv
    gvb = gv.astype(BF16)
    row = lax.broadcasted_iota(jnp.int32, (R, R), 0)
    col = lax.broadcasted_iota(jnp.int32, (R, R), 1)
    live = (col <= row) & (col >= (row // L) * L)
    bms = []
    for g in range(GMLP_GROUPS):
        wm = jnp.where(live, wst_ref[g], 0.0).astype(BF16)
        c0 = g * GMLP_GW
        mix = _dot(wm, gvb[:, c0:c0 + GMLP_GW]) + bst_ref[:, g:g + 1]
        bms.append(u[:, c0:c0 + GMLP_GW] * mix)
    bm = jnp.concatenate(bms, axis=-1).astype(BF16)

    gates = _dot(h, win_ref[:, OFF_GA:IN_W])
    merged = (jax.nn.sigmoid(gates[:, 0:D_MODEL]) * _dot(a_s[...], wap_ref[...])
              + jax.nn.sigmoid(gates[:, D_MODEL:2 * D_MODEL]) * _dot(bm, wgp_ref[...]))
    mo = _dot(merged.astype(BF16), wout_ref[...])
    x1_ref[...] = x + _rms(mo, npost_ref[...])


def _mixer_sample(x, ck, cv, npre, npost, win, sinks, lng, lnb, wst, bst, wap, wgp, wout):
    R = x.shape[0]
    NB = ck.shape[0]
    vmem = functools.partial(pl.BlockSpec, memory_space=pltpu.VMEM)
    return pl.pallas_call(
        _mixer_sample_kernel,
        in_specs=[vmem(), vmem(), vmem(), vmem(), vmem(), vmem(),
                  pl.BlockSpec(memory_space=pltpu.SMEM),
                  vmem(), vmem(), vmem(), vmem(), vmem(), vmem(), vmem()],
        out_specs=[vmem(), vmem(), vmem(), vmem()],
        out_shape=[
            jax.ShapeDtypeStruct((R, D_MODEL), F32),
            jax.ShapeDtypeStruct((NB, WINDOW, KV_W), F32),
            jax.ShapeDtypeStruct((NB, WINDOW, KV_W), F32),
            jax.ShapeDtypeStruct((R, GMLP_W), F32),
        ],
        scratch_shapes=[
            pltpu.VMEM((R, ATTN_W), BF16),
            pltpu.VMEM((R, KV_W), F32),
            pltpu.VMEM((R, KV_W), F32),
            pltpu.VMEM((R, ATTN_W), BF16),
        ],
        compiler_params=pltpu.CompilerParams(vmem_limit_bytes=VMEM_LIMIT_BYTES),
        name="mixer_sample",
    )(x, ck, cv, npre, npost, win, sinks, lng, lnb, wst, bst, wap, wgp, wout)


def _ffn_kernel(x_ref, npre_ref, npost_ref, wfi_ref, wfo_ref, y_ref):
    x = x_ref[...]
    h = _rms(x, npre_ref[...]).astype(BF16)
    gate = _dot(h, wfi_ref[:, 0:D_FF])
    up = _dot(h, wfi_ref[:, D_FF:2 * D_FF])
    act = (jax.nn.silu(gate) * up).astype(BF16)
    y_ref[...] = x + _rms(_dot(act, wfo_ref[...]), npost_ref[...])


def _ffn(x, npre, npost, wfi, wfo):
    R = x.shape[0]
    T = min(FFN_TILE, R)
    assert R % T == 0
    row_spec = pl.BlockSpec((T, D_MODEL), lambda i: (i, 0))
    return pl.pallas_call(
        _ffn_kernel,
        grid=(R // T,),
        in_specs=[row_spec, _resident((1, D_MODEL)), _resident((1, D_MODEL)),
                  _resident((D_MODEL, 2 * D_FF)), _resident((D_FF, D_MODEL))],
        out_specs=row_spec,
        out_shape=jax.ShapeDtypeStruct((R, D_MODEL), F32),
        compiler_params=pltpu.CompilerParams(
            dimension_semantics=("arbitrary",),
            vmem_limit_bytes=VMEM_LIMIT_BYTES),
        name="ffn",
    )(x, npre, npost, wfi, wfo)


def kernel(x_prompt, x_sample, cache_attn_k, cache_attn_v, norm_pre_mix, norm_post_mix, norm_pre_ffn,
           norm_post_ffn, w_in, attn_sinks, gmlp_ln_g, gmlp_ln_b, gmlp_w_s, gmlp_b_s, w_attn_proj,
           w_gmlp_proj, w_out, w_ffn_in, w_ffn_out):
    depth = w_in.shape[0]
    B, S, _ = x_prompt.shape
    NB, L, _ = x_sample.shape
    xp = x_prompt
    xs = x_sample.reshape(NB * L, D_MODEL)
    kp_l, vp_l, ks_l, vs_l, gs_l = [], [], [], [], []
    for l in range(depth):
        npre, npost = norm_pre_mix[l][None], norm_post_mix[l][None]
        fpre, fpost = norm_pre_ffn[l][None], norm_post_ffn[l][None]
        lng, lnb = gmlp_ln_g[l][None], gmlp_ln_b[l][None]
        win = w_in[l].astype(BF16)
        wap = w_attn_proj[l].astype(BF16)
        wgp = w_gmlp_proj[l].astype(BF16)
        wout = w_out[l].astype(BF16)
        wfi = w_ffn_in[l].astype(BF16)
        wfo = w_ffn_out[l].astype(BF16)
        ws = gmlp_w_s[l]
        bst = gmlp_b_s[l].T
        wst = jnp.tile(ws[:, :L, :L], (1, NB, NB))
        bst_s = jnp.tile(bst[:L], (NB, 1))

        x1p, kp, vp = _mixer_prompt(xp, npre, npost, win, attn_sinks[l], lng, lnb, ws, bst,
                                    wap, wgp, wout)
        xp = _ffn(x1p.reshape(B * S, D_MODEL), fpre, fpost, wfi, wfo).reshape(B, S, D_MODEL)

        ck = cache_attn_k[l].reshape(NB, WINDOW, KV_W)
        cv = cache_attn_v[l].reshape(NB, WINDOW, KV_W)
        x1s, ks, vs, gvs = _mixer_sample(xs, ck, cv, npre, npost, win, attn_sinks[l], lng, lnb,
                                         wst, bst_s, wap, wgp, wout)
        xs = _ffn(x1s, fpre, fpost, wfi, wfo)

        kp_l.append(kp.reshape(B, WINDOW, N_KV_HEADS, HEAD_DIM))
        vp_l.append(vp.reshape(B, WINDOW, N_KV_HEADS, HEAD_DIM))
        ks_l.append(ks.reshape(NB, WINDOW, N_KV_HEADS, HEAD_DIM))
        vs_l.append(vs.reshape(NB, WINDOW, N_KV_HEADS, HEAD_DIM))
        gs_l.append(gvs.reshape(NB, L, GMLP_W))
    return (xp, xs.reshape(NB, L, D_MODEL), jnp.stack(kp_l), jnp.stack(vp_l), jnp.stack(ks_l),
            jnp.stack(vs_l), jnp.stack(gs_l))
```

```python
import functools

import jax
import jax.numpy as jnp
from jax import lax
from jax.experimental import pallas as pl
from jax.experimental.pallas import tpu as pltpu

D_MODEL = 1024
CHUNK = 64
N_HEADS = 16
N_KV_HEADS = 4
HEAD_DIM = 64
Q_GROUP = N_HEADS // N_KV_HEADS
WINDOW = 128
ATTN_W = N_HEADS * HEAD_DIM
KV_W = N_KV_HEADS * HEAD_DIM
GMLP_CHUNK = 128
GMLP_GROUPS = 4
GMLP_W = D_MODEL
GMLP_GW = GMLP_W // GMLP_GROUPS
D_FF = 2816
IN_W = ATTN_W + 2 * KV_W + 2 * GMLP_W + 2 * D_MODEL
EPS = 1e-6
NEG_INF = -1e30

OFF_K = ATTN_W
OFF_V = OFF_K + KV_W
OFF_U = OFF_V + KV_W
OFF_GV = OFF_U + GMLP_W
OFF_GA = OFF_GV + GMLP_W
OFF_GB = OFF_GA + D_MODEL

KEYS = WINDOW + CHUNK

MIX_TILE = 256
FFN_TILE = 256
VMEM_LIMIT_BYTES = 56 * 1024 * 1024

F32 = jnp.float32
BF16 = jnp.bfloat16


def _dot(a, b):
    return jnp.dot(a, b, preferred_element_type=F32)


def _dot_t(a, b):
    return lax.dot_general(a, b, (((1,), (1,)), ((), ())), preferred_element_type=F32)


def _rms(x, g):
    ms = jnp.mean(x * x, axis=-1, keepdims=True)
    return x * lax.rsqrt(ms + EPS) * g


def _layer_norm(x, g, b):
    mu = jnp.mean(x, axis=-1, keepdims=True)
    xc = x - mu
    var = jnp.mean(xc * xc, axis=-1, keepdims=True)
    return xc * lax.rsqrt(var + EPS) * g + b


def _resident(shape):
    nd = len(shape)
    return pl.BlockSpec(shape, lambda *_: (0,) * nd, pipeline_mode=pl.Buffered(1))


def _kv_head_masks(rows):
    lane = lax.broadcasted_iota(jnp.int32, (rows, KV_W), 1)
    return [(lane >= j * HEAD_DIM) & (lane < (j + 1) * HEAD_DIM) for j in range(N_KV_HEADS)]


def _mixer_prompt_kernel(x_ref, npre_ref, npost_ref, win_ref, sinks_ref, lng_ref, lnb_ref,
                         ws_ref, bst_ref, wap_ref, wgp_ref, wout_ref,
                         x1_ref, kout_ref, vout_ref,
                         kext_s, vext_s, a_s, bm_s):
    t = pl.program_id(1)
    nt = pl.num_programs(1)
    T = MIX_TILE
    n_chunks = T // CHUNK

    @pl.when(t == 0)
    def _():
        kext_s[0:WINDOW, :] = jnp.zeros((WINDOW, KV_W), BF16)
        vext_s[0:WINDOW, :] = jnp.zeros((WINDOW, KV_W), BF16)

    @pl.when(t > 0)
    def _():
        kext_s[0:WINDOW, :] = kext_s[T:T + WINDOW, :]
        vext_s[0:WINDOW, :] = vext_s[T:T + WINDOW, :]

    x = x_ref[...]
    h = _rms(x, npre_ref[...]).astype(BF16)

    qkv = _dot(h, win_ref[:, 0:OFF_U])
    qf = qkv[:, 0:ATTN_W] * (HEAD_DIM ** -0.5)
    k = qkv[:, OFF_K:OFF_V]
    v = qkv[:, OFF_V:OFF_U]
    kext_s[WINDOW:WINDOW + T, :] = k.astype(BF16)
    vext_s[WINDOW:WINDOW + T, :] = v.astype(BF16)

    @pl.when(t == nt - 1)
    def _():
        kout_ref[...] = k[T - WINDOW:T, :]
        vout_ref[...] = v[T - WINDOW:T, :]

    head_mask = _kv_head_masks(CHUNK)
    lane = lax.broadcasted_iota(jnp.int32, (1, KV_W), 1)
    head_keep = [((lane >= j * HEAD_DIM) & (lane < (j + 1) * HEAD_DIM)).astype(F32)
                 for j in range(N_KV_HEADS)]

    def scores(c):
        r0 = c * CHUNK
        kw = kext_s[r0:r0 + KEYS, :]
        out = []
        for half in range(2):
            lhs = []
            for g in range(2 * half, 2 * half + 2):
                qg = qf[r0:r0 + CHUNK, g * KV_W:(g + 1) * KV_W]
                lhs += [(qg * head_keep[j]).astype(BF16) for j in range(N_KV_HEADS)]
            out.append(_dot_t(jnp.concatenate(lhs, axis=0), kw))
        return out

    def softmax(c, s_halves):
        r0 = c * CHUNK
        if r0 < WINDOW:
            first_valid = jnp.where(t == 0, WINDOW - r0, 0)
            valid = lax.broadcasted_iota(jnp.int32, (CHUNK, KEYS), 1) >= first_valid
        else:
            valid = None
        p_halves, invs = [], {}
        for half in range(2):
            ps = []
            for gl in range(2):
                g = 2 * half + gl
                for j in range(N_KV_HEADS):
                    i0 = (gl * N_KV_HEADS + j) * CHUNK
                    s = s_halves[half][i0:i0 + CHUNK, :]
                    if valid is not None:
                        s = jnp.where(valid, s, NEG_INF)
                    sk = sinks_ref[j, g]
                    m = jnp.maximum(jnp.max(s, axis=-1, keepdims=True), sk)
                    p = jnp.exp(s - m)
                    denom = jnp.sum(p, axis=-1, keepdims=True) + jnp.exp(sk - m)
                    ps.append(p.astype(BF16))
                    invs[(g, j)] = 1.0 / denom
            p_halves.append(jnp.concatenate(ps, axis=0))
        return p_halves, invs

    def pv(c, p_halves, invs):
        r0 = c * CHUNK
        vw = vext_s[r0:r0 + KEYS, :]
        for half in range(2):
            o = _dot(p_halves[half], vw)
            for gl in range(2):
                g = 2 * half + gl
                acc = None
                for j in range(N_KV_HEADS):
                    i0 = (gl * N_KV_HEADS + j) * CHUNK
                    oj = o[i0:i0 + CHUNK, :] * invs[(g, j)]
                    acc = oj if acc is None else jnp.where(head_mask[j], oj, acc)
                a_s[r0:r0 + CHUNK, g * KV_W:(g + 1) * KV_W] = acc.astype(BF16)

    proj = {}

    def project(name, lo, hi):
        def run():
            proj[name] = _dot(h, win_ref[:, lo:hi])
        return run

    fillers = [project("u", OFF_U, OFF_GV), project("gv", OFF_GV, OFF_GA),
               project("ga", OFF_GA, OFF_GB), project("gb", OFF_GB, IN_W)]

    s_next = scores(0)
    for c in range(n_chunks):
        s_cur = s_next
        if c + 1 < n_chunks:
            s_next = scores(c + 1)
        for f in fillers[c * len(fillers) // n_chunks:(c + 1) * len(fillers) // n_chunks]:
            f()
        p_halves, invs = softmax(c, s_cur)
        pv(c, p_halves, invs)

    u = jax.nn.gelu(proj["u"])
    gv = _layer_norm(jax.nn.gelu(proj["gv"]), lng_ref[...], lnb_ref[...])
    gvb = gv.astype(BF16)
    row = lax.broadcasted_iota(jnp.int32, (GMLP_CHUNK, GMLP_CHUNK), 0)
    col = lax.broadcasted_iota(jnp.int32, (GMLP_CHUNK, GMLP_CHUNK), 1)
    causal = row >= col
    for g in range(GMLP_GROUPS):
        wm = jnp.where(causal, ws_ref[g], 0.0).astype(BF16)
        bias = bst_ref[:, g:g + 1]
        c0 = g * GMLP_GW
        for cc in range(T // GMLP_CHUNK):
            r0 = cc * GMLP_CHUNK
            mix = _dot(wm, gvb[r0:r0 + GMLP_CHUNK, c0:c0 + GMLP_GW]) + bias
            bm_s[r0:r0 + GMLP_CHUNK, c0:c0 + GMLP_GW] = (
                u[r0:r0 + GMLP_CHUNK, c0:c0 + GMLP_GW] * mix).astype(BF16)

    merged = (jax.nn.sigmoid(proj["ga"]) * _dot(a_s[...], wap_ref[...])
              + jax.nn.sigmoid(proj["gb"]) * _dot(bm_s[...], wgp_ref[...]))
    mo = _dot(merged.astype(BF16), wout_ref[...])
    x1_ref[...] = x + _rms(mo, npost_ref[...])


def _mixer_prompt(x, npre, npost, win, sinks, lng, lnb, ws, bst, wap, wgp, wout):
    B, S, _ = x.shape
    T = MIX_TILE
    assert S % T == 0 and T % GMLP_CHUNK == 0 and T >= WINDOW
    row_spec = pl.BlockSpec((None, T, D_MODEL), lambda b, t: (b, t, 0))
    kv_spec = pl.BlockSpec((None, WINDOW, KV_W), lambda b, t: (b, 0, 0))
    return pl.pallas_call(
        _mixer_prompt_kernel,
        grid=(B, S // T),
        in_specs=[
            row_spec,
            _resident((1, D_MODEL)), _resident((1, D_MODEL)),
            _resident((D_MODEL, IN_W)),
            pl.BlockSpec(memory_space=pltpu.SMEM),
            _resident((1, GMLP_W)), _resident((1, GMLP_W)),
            _resident((GMLP_GROUPS, GMLP_CHUNK, GMLP_CHUNK)),
            _resident((GMLP_CHUNK, GMLP_GROUPS)),
            _resident((ATTN_W, D_MODEL)), _resident((GMLP_W, D_MODEL)),
            _resident((D_MODEL, D_MODEL)),
        ],
        out_specs=[row_spec, kv_spec, kv_spec],
        out_shape=[
            jax.ShapeDtypeStruct((B, S, D_MODEL), F32),
            jax.ShapeDtypeStruct((B, WINDOW, KV_W), F32),
            jax.ShapeDtypeStruct((B, WINDOW, KV_W), F32),
        ],
        scratch_shapes=[
            pltpu.VMEM((WINDOW + T, KV_W), BF16),
            pltpu.VMEM((WINDOW + T, KV_W), BF16),
            pltpu.VMEM((T, ATTN_W), BF16),
            pltpu.VMEM((T, GMLP_W), BF16),
        ],
        compiler_params=pltpu.CompilerParams(
            dimension_semantics=("arbitrary", "arbitrary"),
            vmem_limit_bytes=VMEM_LIMIT_BYTES),
        name="mixer_prompt",
    )(x, npre, npost, win, sinks, lng, lnb, ws, bst, wap, wgp, wout)


def _mixer_sample_kernel(x_ref, ck_ref, cv_ref, npre_ref, npost_ref, win_ref, sinks_ref,
                         lng_ref, lnb_ref, wst_ref, bst_ref, wap_ref, wgp_ref, wout_ref,
                         x1_ref, kout_ref, vout_ref, gv_ref,
                         q_s, k_s, v_s, a_s):
    NB, L = ck_ref.shape[0], x_ref.shape[0] // ck_ref.shape[0]
    R = NB * L
    x = x_ref[...]
    h = _rms(x, npre_ref[...]).astype(BF16)

    qkv = _dot(h, win_ref[:, 0:OFF_U])
    q_s[...] = (qkv[:, 0:ATTN_W] * (HEAD_DIM ** -0.5)).astype(BF16)
    k_s[...] = qkv[:, OFF_K:OFF_V]
    v_s[...] = qkv[:, OFF_V:OFF_U]

    def per_stream(b, carry):
        r0 = pl.multiple_of(b * L, L)
        knew = k_s[pl.ds(r0, L), :]
        vnew = v_s[pl.ds(r0, L), :]
        kc = ck_ref[b]
        vc = cv_ref[b]
        kout_ref[b, 0:WINDOW - L, :] = kc[L:WINDOW, :]
        kout_ref[b, WINDOW - L:WINDOW, :] = knew
        vout_ref[b, 0:WINDOW - L, :] = vc[L:WINDOW, :]
        vout_ref[b, WINDOW - L:WINDOW, :] = vnew
        kcb, vcb = kc.astype(BF16), vc.astype(BF16)
        knb, vnb = knew.astype(BF16), vnew.astype(BF16)
        qb = q_s[pl.ds(r0, L), :]
        outs = []
        for g in range(Q_GROUP):
            for j in range(N_KV_HEADS):
                cs = slice(j * HEAD_DIM, (j + 1) * HEAD_DIM)
                q0 = (g * N_KV_HEADS + j) * HEAD_DIM
                qh = qb[:, q0:q0 + HEAD_DIM]
                s1 = _dot_t(qh, kcb[:, cs])
                s2 = _dot_t(qh, knb[:, cs])
                sk = sinks_ref[j, g]
                m = jnp.maximum(jnp.maximum(jnp.max(s1, axis=-1, keepdims=True),
                                            jnp.max(s2, axis=-1, keepdims=True)), sk)
                p1 = jnp.exp(s1 - m)
                p2 = jnp.exp(s2 - m)
                denom = (jnp.sum(p1, axis=-1, keepdims=True)
                         + jnp.sum(p2, axis=-1, keepdims=True) + jnp.exp(sk - m))
                o = (_dot(p1.astype(BF16), vcb[:, cs]) + _dot(p2.astype(BF16), vnb[:, cs])) / denom
                outs.append(o)
        a_s[pl.ds(r0, L), :] = jnp.concatenate(outs, axis=-1).astype(BF16)
        return carry

    lax.fori_loop(0, NB, per_stream, 0)

    zg = _dot(h, win_ref[:, OFF_U:OFF_GA])
    u = jax.nn.gelu(zg[:, 0:GMLP_W])
    gv = _layer_norm(jax.nn.gelu(zg[:, GMLP_W:2 * GMLP_W]), lng_ref[...], lnb_ref[...])
    gv_ref[...] = gv
    gvb = gv.astype(BF16)
    row = lax.broadcasted_iota(jnp.int32, (R, R), 0)
    col = lax.broadcasted_iota(jnp.int32, (R, R), 1)
    live = (col <= row) & (col >= (row // L) * L)
    bms = []
    for g in range(GMLP_GROUPS):
        wm = jnp.where(live, wst_ref[g], 0.0).astype(BF16)
        c0 = g * GMLP_GW
        mix = _dot(wm, gvb[:, c0:c0 + GMLP_GW]) + bst_ref[:, g:g + 1]
        bms.append(u[:, c0:c0 + GMLP_GW] * mix)
    bm = jnp.concatenate(bms, axis=-1).astype(BF16)

    gates = _dot(h, win_ref[:, OFF_GA:IN_W])
    merged = (jax.nn.sigmoid(gates[:, 0:D_MODEL]) * _dot(a_s[...], wap_ref[...])
              + jax.nn.sigmoid(gates[:, D_MODEL:2 * D_MODEL]) * _dot(bm, wgp_ref[...]))
    mo = _dot(merged.astype(BF16), wout_ref[...])
    x1_ref[...] = x + _rms(mo, npost_ref[...])


def _mixer_sample(x, ck, cv, npre, npost, win, sinks, lng, lnb, wst, bst, wap, wgp, wout):
    R = x.shape[0]
    NB = ck.shape[0]
    vmem = functools.partial(pl.BlockSpec, memory_space=pltpu.VMEM)
    return pl.pallas_call(
        _mixer_sample_kernel,
        in_specs=[vmem(), vmem(), vmem(), vmem(), vmem(), vmem(),
                  pl.BlockSpec(memory_space=pltpu.SMEM),
                  vmem(), vmem(), vmem(), vmem(), vmem(), vmem(), vmem()],
        out_specs=[vmem(), vmem(), vmem(), vmem()],
        out_shape=[
            jax.ShapeDtypeStruct((R, D_MODEL), F32),
            jax.ShapeDtypeStruct((NB, WINDOW, KV_W), F32),
            jax.ShapeDtypeStruct((NB, WINDOW, KV_W), F32),
            jax.ShapeDtypeStruct((R, GMLP_W), F32),
        ],
        scratch_shapes=[
            pltpu.VMEM((R, ATTN_W), BF16),
            pltpu.VMEM((R, KV_W), F32),
            pltpu.VMEM((R, KV_W), F32),
            pltpu.VMEM((R, ATTN_W), BF16),
        ],
        compiler_params=pltpu.CompilerParams(vmem_limit_bytes=VMEM_LIMIT_BYTES),
        name="mixer_sample",
    )(x, ck, cv, npre, npost, win, sinks, lng, lnb, wst, bst, wap, wgp, wout)


def _ffn_kernel(x_ref, npre_ref, npost_ref, wfi_ref, wfo_ref, y_ref):
    x = x_ref[...]
    h = _rms(x, npre_ref[...]).astype(BF16)
    gate = _dot(h, wfi_ref[:, 0:D_FF])
    up = _dot(h, wfi_ref[:, D_FF:2 * D_FF])
    act = (jax.nn.silu(gate) * up).astype(BF16)
    y_ref[...] = x + _rms(_dot(act, wfo_ref[...]), npost_ref[...])


def _ffn(x, npre, npost, wfi, wfo):
    R = x.shape[0]
    T = min(FFN_TILE, R)
    assert R % T == 0
    row_spec = pl.BlockSpec((T, D_MODEL), lambda i: (i, 0))
    return pl.pallas_call(
        _ffn_kernel,
        grid=(R // T,),
        in_specs=[row_spec, _resident((1, D_MODEL)), _resident((1, D_MODEL)),
                  _resident((D_MODEL, 2 * D_FF)), _resident((D_FF, D_MODEL))],
        out_specs=row_spec,
        out_shape=jax.ShapeDtypeStruct((R, D_MODEL), F32),
        compiler_params=pltpu.CompilerParams(
            dimension_semantics=("arbitrary",),
            vmem_limit_bytes=VMEM_LIMIT_BYTES),
        name="ffn",
    )(x, npre, npost, wfi, wfo)


def _group_major(w, axis):
    shape = w.shape
    split = shape[:axis] + (N_KV_HEADS, Q_GROUP, HEAD_DIM) + shape[axis + 1:]
    return jnp.swapaxes(w.reshape(split), axis, axis + 1).reshape(shape)


def kernel(x_prompt, x_sample, cache_attn_k, cache_attn_v, norm_pre_mix, norm_post_mix, norm_pre_ffn,
           norm_post_ffn, w_in, attn_sinks, gmlp_ln_g, gmlp_ln_b, gmlp_w_s, gmlp_b_s, w_attn_proj,
           w_gmlp_proj, w_out, w_ffn_in, w_ffn_out):
    depth = w_in.shape[0]
    B, S, _ = x_prompt.shape
    NB, L, _ = x_sample.shape
    xp = x_prompt
    xs = x_sample.reshape(NB * L, D_MODEL)
    kp_l, vp_l, ks_l, vs_l, gs_l = [], [], [], [], []
    for l in range(depth):
        npre, npost = norm_pre_mix[l][None], norm_post_mix[l][None]
        fpre, fpost = norm_pre_ffn[l][None], norm_post_ffn[l][None]
        lng, lnb = gmlp_ln_g[l][None], gmlp_ln_b[l][None]
        win = jnp.concatenate([_group_major(w_in[l][:, :ATTN_W], 1), w_in[l][:, ATTN_W:]],
                              axis=1).astype(BF16)
        wap = _group_major(w_attn_proj[l], 0).astype(BF16)
        wgp = w_gmlp_proj[l].astype(BF16)
        wout = w_out[l].astype(BF16)
        wfi = w_ffn_in[l].astype(BF16)
        wfo = w_ffn_out[l].astype(BF16)
        ws = gmlp_w_s[l]
        bst = gmlp_b_s[l].T
        wst = jnp.tile(ws[:, :L, :L], (1, NB, NB))
        bst_s = jnp.tile(bst[:L], (NB, 1))

        x1p, kp, vp = _mixer_prompt(xp, npre, npost, win, attn_sinks[l], lng, lnb, ws, bst,
                                    wap, wgp, wout)
        xp = _ffn(x1p.reshape(B * S, D_MODEL), fpre, fpost, wfi, wfo).reshape(B, S, D_MODEL)

        ck = cache_attn_k[l].reshape(NB, WINDOW, KV_W)
        cv = cache_attn_v[l].reshape(NB, WINDOW, KV_W)
        x1s, ks, vs, gvs = _mixer_sample(xs, ck, cv, npre, npost, win, attn_sinks[l], lng, lnb,
                                         wst, bst_s, wap, wgp, wout)
        xs = _ffn(x1s, fpre, fpost, wfi, wfo)

        kp_l.append(kp.reshape(B, WINDOW, N_KV_HEADS, HEAD_DIM))
        vp_l.append(vp.reshape(B, WINDOW, N_KV_HEADS, HEAD_DIM))
        ks_l.append(ks.reshape(NB, WINDOW, N_KV_HEADS, HEAD_DIM))
        vs_l.append(vs.reshape(NB, WINDOW, N_KV_HEADS, HEAD_DIM))
        gs_l.append(gvs.reshape(NB, L, GMLP_W))
    return (xp, xs.reshape(NB, L, D_MODEL), jnp.stack(kp_l), jnp.stack(vp_l), jnp.stack(ks_l),
            jnp.stack(vs_l), jnp.stack(gs_l))
```

```python
import functools

import jax
import jax.numpy as jnp
from jax import lax
from jax.experimental import pallas as pl
from jax.experimental.pallas import tpu as pltpu

D_MODEL = 1024
CHUNK = 64
N_HEADS = 16
N_KV_HEADS = 4
HEAD_DIM = 64
Q_GROUP = N_HEADS // N_KV_HEADS
WINDOW = 128
ATTN_W = N_HEADS * HEAD_DIM
KV_W = N_KV_HEADS * HEAD_DIM
GMLP_CHUNK = 128
GMLP_GROUPS = 4
GMLP_W = D_MODEL
GMLP_GW = GMLP_W // GMLP_GROUPS
D_FF = 2816
IN_W = ATTN_W + 2 * KV_W + 2 * GMLP_W + 2 * D_MODEL
EPS = 1e-6
NEG_INF = -1e30

OFF_K = ATTN_W
OFF_V = OFF_K + KV_W
OFF_U = OFF_V + KV_W
OFF_GV = OFF_U + GMLP_W
OFF_GA = OFF_GV + GMLP_W
OFF_GB = OFF_GA + D_MODEL

KEYS = WINDOW + CHUNK

MIX_TILE = 512
FFN_TILE = 512
VMEM_LIMIT_BYTES = 56 * 1024 * 1024

F32 = jnp.float32
BF16 = jnp.bfloat16


def _dot(a, b):
    return jnp.dot(a, b, preferred_element_type=F32)


def _dot_t(a, b):
    return lax.dot_general(a, b, (((1,), (1,)), ((), ())), preferred_element_type=F32)


def _rms(x, g):
    ms = jnp.mean(x * x, axis=-1, keepdims=True)
    return x * lax.rsqrt(ms + EPS) * g


def _layer_norm(x, g, b):
    mu = jnp.mean(x, axis=-1, keepdims=True)
    xc = x - mu
    var = jnp.mean(xc * xc, axis=-1, keepdims=True)
    return xc * lax.rsqrt(var + EPS) * g + b


def _resident(shape):
    nd = len(shape)
    return pl.BlockSpec(shape, lambda *_: (0,) * nd, pipeline_mode=pl.Buffered(1))


def _kv_head_masks(rows):
    lane = lax.broadcasted_iota(jnp.int32, (rows, KV_W), 1)
    return [(lane >= j * HEAD_DIM) & (lane < (j + 1) * HEAD_DIM) for j in range(N_KV_HEADS)]


def _mixer_prompt_kernel(x_ref, npre_ref, npost_ref, win_ref, sinks_ref, lng_ref, lnb_ref,
                         ws_ref, bst_ref, wap_ref, wgp_ref, wout_ref,
                         x1_ref, kout_ref, vout_ref,
                         kext_s, vext_s, a_s, bm_s):
    t = pl.program_id(1)
    nt = pl.num_programs(1)
    T = MIX_TILE
    n_chunks = T // CHUNK

    @pl.when(t == 0)
    def _():
        kext_s[0:WINDOW, :] = jnp.zeros((WINDOW, KV_W), BF16)
        vext_s[0:WINDOW, :] = jnp.zeros((WINDOW, KV_W), BF16)

    @pl.when(t > 0)
    def _():
        kext_s[0:WINDOW, :] = kext_s[T:T + WINDOW, :]
        vext_s[0:WINDOW, :] = vext_s[T:T + WINDOW, :]

    x = x_ref[...]
    h = _rms(x, npre_ref[...]).astype(BF16)

    qkv = _dot(h, win_ref[:, 0:OFF_U])
    qf = qkv[:, 0:ATTN_W] * (HEAD_DIM ** -0.5)
    k = qkv[:, OFF_K:OFF_V]
    v = qkv[:, OFF_V:OFF_U]
    kext_s[WINDOW:WINDOW + T, :] = k.astype(BF16)
    vext_s[WINDOW:WINDOW + T, :] = v.astype(BF16)

    @pl.when(t == nt - 1)
    def _():
        kout_ref[...] = k[T - WINDOW:T, :]
        vout_ref[...] = v[T - WINDOW:T, :]

    head_mask = _kv_head_masks(CHUNK)
    lane = lax.broadcasted_iota(jnp.int32, (1, KV_W), 1)
    head_keep = [((lane >= j * HEAD_DIM) & (lane < (j + 1) * HEAD_DIM)).astype(F32)
                 for j in range(N_KV_HEADS)]

    def scores(c):
        r0 = c * CHUNK
        kw = kext_s[r0:r0 + KEYS, :]
        out = []
        for half in range(2):
            lhs = []
            for g in range(2 * half, 2 * half + 2):
                qg = qf[r0:r0 + CHUNK, g * KV_W:(g + 1) * KV_W]
                lhs += [(qg * head_keep[j]).astype(BF16) for j in range(N_KV_HEADS)]
            out.append(_dot_t(jnp.concatenate(lhs, axis=0), kw))
        return out

    def softmax(c, s_halves):
        r0 = c * CHUNK
        if r0 < WINDOW:
            first_valid = jnp.where(t == 0, WINDOW - r0, 0)
            valid = lax.broadcasted_iota(jnp.int32, (CHUNK, KEYS), 1) >= first_valid
        else:
            valid = None
        p_halves, invs = [], {}
        for half in range(2):
            ps = []
            for gl in range(2):
                g = 2 * half + gl
                for j in range(N_KV_HEADS):
                    i0 = (gl * N_KV_HEADS + j) * CHUNK
                    s = s_halves[half][i0:i0 + CHUNK, :]
                    if valid is not None:
                        s = jnp.where(valid, s, NEG_INF)
                    sk = sinks_ref[j, g]
                    m = jnp.maximum(jnp.max(s, axis=-1, keepdims=True), sk)
                    p = jnp.exp(s - m)
                    denom = jnp.sum(p, axis=-1, keepdims=True) + jnp.exp(sk - m)
                    ps.append(p.astype(BF16))
                    invs[(g, j)] = 1.0 / denom
            p_halves.append(jnp.concatenate(ps, axis=0))
        return p_halves, invs

    def pv(c, p_halves, invs):
        r0 = c * CHUNK
        vw = vext_s[r0:r0 + KEYS, :]
        for half in range(2):
            o = _dot(p_halves[half], vw)
            for gl in range(2):
                g = 2 * half + gl
                acc = None
                for j in range(N_KV_HEADS):
                    i0 = (gl * N_KV_HEADS + j) * CHUNK
                    oj = o[i0:i0 + CHUNK, :] * invs[(g, j)]
                    acc = oj if acc is None else jnp.where(head_mask[j], oj, acc)
                a_s[r0:r0 + CHUNK, g * KV_W:(g + 1) * KV_W] = acc.astype(BF16)

    rest_w = (IN_W - OFF_U) // n_chunks
    rest = []

    s_next = scores(0)
    for c in range(n_chunks):
        s_cur = s_next
        if c + 1 < n_chunks:
            s_next = scores(c + 1)
        rest.append(_dot(h, win_ref[:, OFF_U + c * rest_w:OFF_U + (c + 1) * rest_w]))
        p_halves, invs = softmax(c, s_cur)
        pv(c, p_halves, invs)
    rest = jnp.concatenate(rest, axis=1)
    proj = {"u": rest[:, 0:GMLP_W], "gv": rest[:, GMLP_W:2 * GMLP_W],
            "ga": rest[:, 2 * GMLP_W:2 * GMLP_W + D_MODEL],
            "gb": rest[:, 2 * GMLP_W + D_MODEL:2 * GMLP_W + 2 * D_MODEL]}

    u = jax.nn.gelu(proj["u"])
    gv = _layer_norm(jax.nn.gelu(proj["gv"]), lng_ref[...], lnb_ref[...])
    gvb = gv.astype(BF16)
    row = lax.broadcasted_iota(jnp.int32, (GMLP_CHUNK, GMLP_CHUNK), 0)
    col = lax.broadcasted_iota(jnp.int32, (GMLP_CHUNK, GMLP_CHUNK), 1)
    causal = row >= col
    for g in range(GMLP_GROUPS):
        wm = jnp.where(causal, ws_ref[g], 0.0).astype(BF16)
        bias = bst_ref[:, g:g + 1]
        c0 = g * GMLP_GW
        for cc in range(T // GMLP_CHUNK):
            r0 = cc * GMLP_CHUNK
            mix = _dot(wm, gvb[r0:r0 + GMLP_CHUNK, c0:c0 + GMLP_GW]) + bias
            bm_s[r0:r0 + GMLP_CHUNK, c0:c0 + GMLP_GW] = (
                u[r0:r0 + GMLP_CHUNK, c0:c0 + GMLP_GW] * mix).astype(BF16)

    merged = (jax.nn.sigmoid(proj["ga"]) * _dot(a_s[...], wap_ref[...])
              + jax.nn.sigmoid(proj["gb"]) * _dot(bm_s[...], wgp_ref[...]))
    mo = _dot(merged.astype(BF16), wout_ref[...])
    x1_ref[...] = x + _rms(mo, npost_ref[...])


def _mixer_prompt(x, npre, npost, win, sinks, lng, lnb, ws, bst, wap, wgp, wout):
    B, S, _ = x.shape
    T = MIX_TILE
    assert S % T == 0 and T % GMLP_CHUNK == 0 and T >= WINDOW
    row_spec = pl.BlockSpec((None, T, D_MODEL), lambda b, t: (b, t, 0))
    kv_spec = pl.BlockSpec((None, WINDOW, KV_W), lambda b, t: (b, 0, 0))
    return pl.pallas_call(
        _mixer_prompt_kernel,
        grid=(B, S // T),
        in_specs=[
            row_spec,
            _resident((1, D_MODEL)), _resident((1, D_MODEL)),
            _resident((D_MODEL, IN_W)),
            pl.BlockSpec(memory_space=pltpu.SMEM),
            _resident((1, GMLP_W)), _resident((1, GMLP_W)),
            _resident((GMLP_GROUPS, GMLP_CHUNK, GMLP_CHUNK)),
            _resident((GMLP_CHUNK, GMLP_GROUPS)),
            _resident((ATTN_W, D_MODEL)), _resident((GMLP_W, D_MODEL)),
            _resident((D_MODEL, D_MODEL)),
        ],
        out_specs=[row_spec, kv_spec, kv_spec],
        out_shape=[
            jax.ShapeDtypeStruct((B, S, D_MODEL), F32),
            jax.ShapeDtypeStruct((B, WINDOW, KV_W), F32),
            jax.ShapeDtypeStruct((B, WINDOW, KV_W), F32),
        ],
        scratch_shapes=[
            pltpu.VMEM((WINDOW + T, KV_W), BF16),
            pltpu.VMEM((WINDOW + T, KV_W), BF16),
            pltpu.VMEM((T, ATTN_W), BF16),
            pltpu.VMEM((T, GMLP_W), BF16),
        ],
        compiler_params=pltpu.CompilerParams(
            dimension_semantics=("arbitrary", "arbitrary"),
            vmem_limit_bytes=VMEM_LIMIT_BYTES),
        name="mixer_prompt",
    )(x, npre, npost, win, sinks, lng, lnb, ws, bst, wap, wgp, wout)


def _mixer_sample_kernel(x_ref, ck_ref, cv_ref, npre_ref, npost_ref, win_ref, sinks_ref,
                         lng_ref, lnb_ref, wst_ref, bst_ref, wap_ref, wgp_ref, wout_ref,
                         x1_ref, kout_ref, vout_ref, gv_ref,
                         q_s, k_s, v_s, a_s):
    NB, L = ck_ref.shape[0], x_ref.shape[0] // ck_ref.shape[0]
    R = NB * L
    x = x_ref[...]
    h = _rms(x, npre_ref[...]).astype(BF16)

    qkv = _dot(h, win_ref[:, 0:OFF_U])
    q_s[...] = (qkv[:, 0:ATTN_W] * (HEAD_DIM ** -0.5)).astype(BF16)
    k_s[...] = qkv[:, OFF_K:OFF_V]
    v_s[...] = qkv[:, OFF_V:OFF_U]

    def per_stream(b, carry):
        r0 = pl.multiple_of(b * L, L)
        knew = k_s[pl.ds(r0, L), :]
        vnew = v_s[pl.ds(r0, L), :]
        kc = ck_ref[b]
        vc = cv_ref[b]
        kout_ref[b, 0:WINDOW - L, :] = kc[L:WINDOW, :]
        kout_ref[b, WINDOW - L:WINDOW, :] = knew
        vout_ref[b, 0:WINDOW - L, :] = vc[L:WINDOW, :]
        vout_ref[b, WINDOW - L:WINDOW, :] = vnew
        kcb, vcb = kc.astype(BF16), vc.astype(BF16)
        knb, vnb = knew.astype(BF16), vnew.astype(BF16)
        qb = q_s[pl.ds(r0, L), :]
        outs = []
        for g in range(Q_GROUP):
            for j in range(N_KV_HEADS):
                cs = slice(j * HEAD_DIM, (j + 1) * HEAD_DIM)
                q0 = (g * N_KV_HEADS + j) * HEAD_DIM
                qh = qb[:, q0:q0 + HEAD_DIM]
                s1 = _dot_t(qh, kcb[:, cs])
                s2 = _dot_t(qh, knb[:, cs])
                sk = sinks_ref[j, g]
                m = jnp.maximum(jnp.maximum(jnp.max(s1, axis=-1, keepdims=True),
                                            jnp.max(s2, axis=-1, keepdims=True)), sk)
                p1 = jnp.exp(s1 - m)
                p2 = jnp.exp(s2 - m)
                denom = (jnp.sum(p1, axis=-1, keepdims=True)
                         + jnp.sum(p2, axis=-1, keepdims=True) + jnp.exp(sk - m))
                o = (_dot(p1.astype(BF16), vcb[:, cs]) + _dot(p2.astype(BF16), vnb[:, cs])) / denom
                outs.append(o)
        a_s[pl.ds(r0, L), :] = jnp.concatenate(outs, axis=-1).astype(BF16)
        return carry

    lax.fori_loop(0, NB, per_stream, 0)

    zg = _dot(h, win_ref[:, OFF_U:OFF_GA])
    u = jax.nn.gelu(zg[:, 0:GMLP_W])
    gv = _layer_norm(jax.nn.gelu(zg[:, GMLP_W:2 * GMLP_W]), lng_ref[...], lnb_ref[...])
    gv_ref[...] = gv
    gvb = gv.astype(BF16)
    row = lax.broadcasted_iota(jnp.int32, (R, R), 0)
    col = lax.broadcasted_iota(jnp.int32, (R, R), 1)
    live = (col <= row) & (col >= (row // L) * L)
    bms = []
    for g in range(GMLP_GROUPS):
        wm = jnp.where(live, wst_ref[g], 0.0).astype(BF16)
        c0 = g * GMLP_GW
        mix = _dot(wm, gvb[:, c0:c0 + GMLP_GW]) + bst_ref[:, g:g + 1]
        bms.append(u[:, c0:c0 + GMLP_GW] * mix)
    bm = jnp.concatenate(bms, axis=-1).astype(BF16)

    gates = _dot(h, win_ref[:, OFF_GA:IN_W])
    merged = (jax.nn.sigmoid(gates[:, 0:D_MODEL]) * _dot(a_s[...], wap_ref[...])
              + jax.nn.sigmoid(gates[:, D_MODEL:2 * D_MODEL]) * _dot(bm, wgp_ref[...]))
    mo = _dot(merged.astype(BF16), wout_ref[...])
    x1_ref[...] = x + _rms(mo, npost_ref[...])


def _mixer_sample(x, ck, cv, npre, npost, win, sinks, lng, lnb, wst, bst, wap, wgp, wout):
    R = x.shape[0]
    NB = ck.shape[0]
    vmem = functools.partial(pl.BlockSpec, memory_space=pltpu.VMEM)
    return pl.pallas_call(
        _mixer_sample_kernel,
        in_specs=[vmem(), vmem(), vmem(), vmem(), vmem(), vmem(),
                  pl.BlockSpec(memory_space=pltpu.SMEM),
                  vmem(), vmem(), vmem(), vmem(), vmem(), vmem(), vmem()],
        out_specs=[vmem(), vmem(), vmem(), vmem()],
        out_shape=[
            jax.ShapeDtypeStruct((R, D_MODEL), F32),
            jax.ShapeDtypeStruct((NB, WINDOW, KV_W), F32),
            jax.ShapeDtypeStruct((NB, WINDOW, KV_W), F32),
            jax.ShapeDtypeStruct((R, GMLP_W), F32),
        ],
        scratch_shapes=[
            pltpu.VMEM((R, ATTN_W), BF16),
            pltpu.VMEM((R, KV_W), F32),
            pltpu.VMEM((R, KV_W), F32),
            pltpu.VMEM((R, ATTN_W), BF16),
        ],
        compiler_params=pltpu.CompilerParams(vmem_limit_bytes=VMEM_LIMIT_BYTES),
        name="mixer_sample",
    )(x, ck, cv, npre, npost, win, sinks, lng, lnb, wst, bst, wap, wgp, wout)


def _ffn_kernel(x_ref, npre_ref, npost_ref, wfi_ref, wfo_ref, y_ref):
    x = x_ref[...]
    h = _rms(x, npre_ref[...]).astype(BF16)
    gate = _dot(h, wfi_ref[:, 0:D_FF])
    up = _dot(h, wfi_ref[:, D_FF:2 * D_FF])
    act = (jax.nn.silu(gate) * up).astype(BF16)
    y_ref[...] = x + _rms(_dot(act, wfo_ref[...]), npost_ref[...])


def _ffn(x, npre, npost, wfi, wfo):
    R = x.shape[0]
    T = min(FFN_TILE, R)
    assert R % T == 0
    row_spec = pl.BlockSpec((T, D_MODEL), lambda i: (i, 0))
    return pl.pallas_call(
        _ffn_kernel,
        grid=(R // T,),
        in_specs=[row_spec, _resident((1, D_MODEL)), _resident((1, D_MODEL)),
                  _resident((D_MODEL, 2 * D_FF)), _resident((D_FF, D_MODEL))],
        out_specs=row_spec,
        out_shape=jax.ShapeDtypeStruct((R, D_MODEL), F32),
        compiler_params=pltpu.CompilerParams(
            dimension_semantics=("arbitrary",),
            vmem_limit_bytes=VMEM_LIMIT_BYTES),
        name="ffn",
    )(x, npre, npost, wfi, wfo)


def _group_major(w, axis):
    shape = w.shape
    split = shape[:axis] + (N_KV_HEADS, Q_GROUP, HEAD_DIM) + shape[axis + 1:]
    return jnp.swapaxes(w.reshape(split), axis, axis + 1).reshape(shape)


def kernel(x_prompt, x_sample, cache_attn_k, cache_attn_v, norm_pre_mix, norm_post_mix, norm_pre_ffn,
           norm_post_ffn, w_in, attn_sinks, gmlp_ln_g, gmlp_ln_b, gmlp_w_s, gmlp_b_s, w_attn_proj,
           w_gmlp_proj, w_out, w_ffn_in, w_ffn_out):
    depth = w_in.shape[0]
    B, S, _ = x_prompt.shape
    NB, L, _ = x_sample.shape
    xp = x_prompt
    xs = x_sample.reshape(NB * L, D_MODEL)
    kp_l, vp_l, ks_l, vs_l, gs_l = [], [], [], [], []
    for l in range(depth):
        npre, npost = norm_pre_mix[l][None], norm_post_mix[l][None]
        fpre, fpost = norm_pre_ffn[l][None], norm_post_ffn[l][None]
        lng, lnb = gmlp_ln_g[l][None], gmlp_ln_b[l][None]
        win = jnp.concatenate([_group_major(w_in[l][:, :ATTN_W], 1), w_in[l][:, ATTN_W:]],
                              axis=1).astype(BF16)
        wap = _group_major(w_attn_proj[l], 0).astype(BF16)
        wgp = w_gmlp_proj[l].astype(BF16)
        wout = w_out[l].astype(BF16)
        wfi = w_ffn_in[l].astype(BF16)
        wfo = w_ffn_out[l].astype(BF16)
        ws = gmlp_w_s[l]
        bst = gmlp_b_s[l].T
        wst = jnp.tile(ws[:, :L, :L], (1, NB, NB))
        bst_s = jnp.tile(bst[:L], (NB, 1))

        x1p, kp, vp = _mixer_prompt(xp, npre, npost, win, attn_sinks[l], lng, lnb, ws, bst,
                                    wap, wgp, wout)
        xp = _ffn(x1p.reshape(B * S, D_MODEL), fpre, fpost, wfi, wfo).reshape(B, S, D_MODEL)

        ck = cache_attn_k[l].reshape(NB, WINDOW, KV_W)
        cv = cache_attn_v[l].reshape(NB, WINDOW, KV_W)
        x1s, ks, vs, gvs = _mixer_sample(xs, ck, cv, npre, npost, win, attn_sinks[l], lng, lnb,
                                         wst, bst_s, wap, wgp, wout)
        xs = _ffn(x1s, fpre, fpost, wfi, wfo)

        kp_l.append(kp.reshape(B, WINDOW, N_KV_HEADS, HEAD_DIM))
        vp_l.append(vp.reshape(B, WINDOW, N_KV_HEADS, HEAD_DIM))
        ks_l.append(ks.reshape(NB, WINDOW, N_KV_HEADS, HEAD_DIM))
        vs_l.append(vs.reshape(NB, WINDOW, N_KV_HEADS, HEAD_DIM))
        gs_l.append(gvs.reshape(NB, L, GMLP_W))
    return (xp, xs.reshape(NB, L, D_MODEL), jnp.stack(kp_l), jnp.stack(vp_l), jnp.stack(ks_l),
            jnp.stack(vs_l), jnp.stack(gs_l))
```

```python
import functools

import jax
import jax.numpy as jnp
from jax import lax
from jax.experimental import pallas as pl
from jax.experimental.pallas import tpu as pltpu

D_MODEL = 1024
CHUNK = 64
N_HEADS = 16
N_KV_HEADS = 4
HEAD_DIM = 64
Q_GROUP = N_HEADS // N_KV_HEADS
WINDOW = 128
ATTN_W = N_HEADS * HEAD_DIM
KV_W = N_KV_HEADS * HEAD_DIM
GMLP_CHUNK = 128
GMLP_GROUPS = 4
GMLP_W = D_MODEL
GMLP_GW = GMLP_W // GMLP_GROUPS
D_FF = 2816
IN_W = ATTN_W + 2 * KV_W + 2 * GMLP_W + 2 * D_MODEL
EPS = 1e-6
NEG_INF = -1e30

REST_W = IN_W - ATTN_W
RO_K = 0
RO_U = 2 * KV_W

KEYS = WINDOW + CHUNK
KEYS_PAD = 256
MXU_N = 256
LOG2E = 1.4426950408889634

MIX_TILE = 1024
MIX_SUB = 512
FFN_TILE = 1024
FFN_SUB = 512
VMEM_LIMIT_BYTES = 56 * 1024 * 1024

F32 = jnp.float32
BF16 = jnp.bfloat16


_dot = functools.partial(jnp.dot, preferred_element_type=F32)
_dot_t = functools.partial(lax.dot_general, dimension_numbers=(((1,), (1,)), ((), ())),
                           preferred_element_type=F32)


def _rms(x, g):
    ms = jnp.mean(x * x, axis=-1, keepdims=True)
    return x * lax.rsqrt(ms + EPS) * g


def _layer_norm(x, g, b):
    mu = jnp.mean(x, axis=-1, keepdims=True)
    xc = x - mu
    var = jnp.mean(xc * xc, axis=-1, keepdims=True)
    return xc * lax.rsqrt(var + EPS) * g + b


def _resident(shape):
    nd = len(shape)
    return pl.BlockSpec(shape, lambda *_: (0,) * nd, pipeline_mode=pl.Buffered(1))


def _kv_head_masks(rows):
    lane = lax.broadcasted_iota(jnp.int32, (rows, KV_W), 1)
    return [(lane >= j * HEAD_DIM) & (lane < (j + 1) * HEAD_DIM) for j in range(N_KV_HEADS)]


def _mixer_sub_tile(base, t, x_ref, npre_ref, npost_ref, wq_ref, wr_ref, sinks_ref, lng_ref, lnb_ref,
                    ws_ref, bst_ref, wap_ref, wgp_ref, wout_ref,
                    x1_ref, kout_ref, vout_ref,
                    kext_s, vext_s, a_s, bm_s):
    T = MIX_SUB
    n_chunks = T // CHUNK

    x = x_ref[base:base + T, :]
    h = _rms(x, npre_ref[...]).astype(BF16)

    qb = (_dot(h, wq_ref[...]) * (HEAD_DIM ** -0.5 * LOG2E)).astype(BF16)
    kv = _dot(h, wr_ref[:, RO_K:RO_U])
    k = kv[:, 0:KV_W]
    v = kv[:, KV_W:2 * KV_W]
    kext_s[WINDOW + base:WINDOW + base + T, :] = k.astype(BF16)
    vext_s[WINDOW + base:WINDOW + base + T, :] = v.astype(BF16)

    if base + T == MIX_TILE:
        kout_ref[...] = k[T - WINDOW:T, :]
        vout_ref[...] = v[T - WINDOW:T, :]

    head_mask = _kv_head_masks(CHUNK)
    window_mask = _kv_head_masks(KEYS)
    zero_q = jnp.zeros((CHUNK, KV_W), BF16)
    zero_w = jnp.zeros((KEYS, KV_W), BF16)
    zero_rows = jnp.zeros((KEYS_PAD - KEYS, KV_W), BF16)
    key_lane = lax.broadcasted_iota(jnp.int32, (CHUNK, KEYS_PAD), 1)

    def scores(c):
        r0 = c * CHUNK
        kw = jnp.concatenate([kext_s[base + r0:base + r0 + KEYS, :], zero_rows], axis=0)
        out = []
        for half in range(2):
            lhs = []
            for g in range(2 * half, 2 * half + 2):
                qg = qb[r0:r0 + CHUNK, g * KV_W:(g + 1) * KV_W]
                lhs += [jnp.where(head_mask[j], qg, zero_q) for j in range(N_KV_HEADS)]
            out.append(_dot_t(jnp.concatenate(lhs, axis=0), kw))
        return out

    def softmax(c, s_halves):
        r0 = base + c * CHUNK
        valid = key_lane < KEYS
        if r0 < WINDOW:
            valid = valid & (key_lane >= jnp.where(t == 0, WINDOW - r0, 0))
        p_halves = []
        for half in range(2):
            rows = []
            for gl in range(2):
                g = 2 * half + gl
                ps = []
                for j in range(N_KV_HEADS):
                    i0 = (gl * N_KV_HEADS + j) * CHUNK
                    s = jnp.where(valid, s_halves[half][i0:i0 + CHUNK, :], NEG_INF)
                    sk = sinks_ref[j, g] * LOG2E
                    m = jnp.maximum(jnp.max(s, axis=-1, keepdims=True), sk)
                    p = jnp.exp2(s - m)
                    denom = jnp.sum(p, axis=-1, keepdims=True) + jnp.exp2(sk - m)
                    ps.append((p * (1.0 / denom)).astype(BF16))
                rows.append(jnp.concatenate(ps, axis=1))
            p_halves.append(jnp.concatenate(rows, axis=0))
        return p_halves

    def pv(c, p_halves):
        r0 = base + c * CHUNK
        vw = vext_s[r0:r0 + KEYS, :]
        blocks = []
        for j in range(N_KV_HEADS):
            blocks += [jnp.where(window_mask[j], vw, zero_w), zero_rows]
        bdv = jnp.concatenate(blocks, axis=0)
        o = _dot(jnp.concatenate(p_halves, axis=0), bdv).astype(BF16)
        for g in range(Q_GROUP):
            a_s[r0:r0 + CHUNK, g * KV_W:(g + 1) * KV_W] = o[g * CHUNK:(g + 1) * CHUNK, :]

    n_tiles = (REST_W - RO_U) // MXU_N
    bounds = [RO_U + MXU_N * (i * n_tiles // (n_chunks + 1)) for i in range(n_chunks + 2)]
    rest = []

    def project_slice(i):
        if bounds[i + 1] > bounds[i]:
            rest.append(_dot(h, wr_ref[:, bounds[i]:bounds[i + 1]]))

    project_slice(0)
    s_next = scores(0)
    for c in range(n_chunks):
        s_cur = s_next
        if c + 1 < n_chunks:
            s_next = scores(c + 1)
        project_slice(c + 1)
        pv(c, softmax(c, s_cur))
    rest = jnp.concatenate(rest, axis=1)
    proj = {"u": rest[:, 0:GMLP_W], "gv": rest[:, GMLP_W:2 * GMLP_W],
            "ga": rest[:, 2 * GMLP_W:2 * GMLP_W + D_MODEL],
            "gb": rest[:, 2 * GMLP_W + D_MODEL:2 * GMLP_W + 2 * D_MODEL]}

    pa = _dot(a_s[base:base + T, :], wap_ref[...])

    u = jax.nn.gelu(proj["u"])
    gv = _layer_norm(jax.nn.gelu(proj["gv"]), lng_ref[...], lnb_ref[...])
    gvb = gv.astype(BF16)
    row = lax.broadcasted_iota(jnp.int32, (GMLP_CHUNK, GMLP_CHUNK), 0)
    col = lax.broadcasted_iota(jnp.int32, (GMLP_CHUNK, GMLP_CHUNK), 1)
    causal = row >= col
    for g in range(GMLP_GROUPS):
        wm = jnp.where(causal, ws_ref[g], 0.0).astype(BF16)
        bias = bst_ref[:, g:g + 1]
        c0 = g * GMLP_GW
        for cc in range(T // GMLP_CHUNK):
            r0 = cc * GMLP_CHUNK
            mix = _dot(wm, gvb[r0:r0 + GMLP_CHUNK, c0:c0 + GMLP_GW]) + bias
            bm_s[base + r0:base + r0 + GMLP_CHUNK, c0:c0 + GMLP_GW] = (
                u[r0:r0 + GMLP_CHUNK, c0:c0 + GMLP_GW] * mix).astype(BF16)

    merged = (jax.nn.sigmoid(proj["ga"]) * pa
              + jax.nn.sigmoid(proj["gb"]) * _dot(bm_s[base:base + T, :], wgp_ref[...]))
    mo = _dot(merged.astype(BF16), wout_ref[...])
    x1_ref[base:base + T, :] = x + _rms(mo, npost_ref[...])


def _mixer_prompt_kernel(x_ref, npre_ref, npost_ref, wq_ref, wr_ref, sinks_ref, lng_ref, lnb_ref,
                         ws_ref, bst_ref, wap_ref, wgp_ref, wout_ref,
                         x1_ref, kout_ref, vout_ref,
                         kext_s, vext_s, a_s, bm_s):
    t = pl.program_id(1)
    T = MIX_TILE

    @pl.when(t == 0)
    def _():
        kext_s[0:WINDOW, :] = jnp.zeros((WINDOW, KV_W), BF16)
        vext_s[0:WINDOW, :] = jnp.zeros((WINDOW, KV_W), BF16)

    @pl.when(t > 0)
    def _():
        kext_s[0:WINDOW, :] = kext_s[T:T + WINDOW, :]
        vext_s[0:WINDOW, :] = vext_s[T:T + WINDOW, :]

    for base in range(0, T, MIX_SUB):
        _mixer_sub_tile(base, t, x_ref, npre_ref, npost_ref, wq_ref, wr_ref, sinks_ref, lng_ref,
                        lnb_ref, ws_ref, bst_ref, wap_ref, wgp_ref, wout_ref,
                        x1_ref, kout_ref, vout_ref, kext_s, vext_s, a_s, bm_s)


def _mixer_prompt(x, npre, npost, wq, wr, sinks, lng, lnb, ws, bst, wap, wgp, wout):
    B, S, _ = x.shape
    T = MIX_TILE
    assert S % T == 0 and T % MIX_SUB == 0 and MIX_SUB % GMLP_CHUNK == 0 and MIX_SUB >= WINDOW
    row_spec = pl.BlockSpec((None, T, D_MODEL), lambda b, t: (b, t, 0))
    kv_spec = pl.BlockSpec((None, WINDOW, KV_W), lambda b, t: (b, 0, 0))
    return pl.pallas_call(
        _mixer_prompt_kernel,
        grid=(B, S // T),
        in_specs=[
            row_spec,
            _resident((1, D_MODEL)), _resident((1, D_MODEL)),
            _resident((D_MODEL, ATTN_W)), _resident((D_MODEL, REST_W)),
            pl.BlockSpec(memory_space=pltpu.SMEM),
            _resident((1, GMLP_W)), _resident((1, GMLP_W)),
            _resident((GMLP_GROUPS, GMLP_CHUNK, GMLP_CHUNK)),
            _resident((GMLP_CHUNK, GMLP_GROUPS)),
            _resident((ATTN_W, D_MODEL)), _resident((GMLP_W, D_MODEL)),
            _resident((D_MODEL, D_MODEL)),
        ],
        out_specs=[row_spec, kv_spec, kv_spec],
        out_shape=[
            jax.ShapeDtypeStruct((B, S, D_MODEL), F32),
            jax.ShapeDtypeStruct((B, WINDOW, KV_W), F32),
            jax.ShapeDtypeStruct((B, WINDOW, KV_W), F32),
        ],
        scratch_shapes=[
            pltpu.VMEM((WINDOW + T, KV_W), BF16),
            pltpu.VMEM((WINDOW + T, KV_W), BF16),
            pltpu.VMEM((T, ATTN_W), BF16),
            pltpu.VMEM((T, GMLP_W), BF16),
        ],
        compiler_params=pltpu.CompilerParams(
            dimension_semantics=("arbitrary", "arbitrary"),
            vmem_limit_bytes=VMEM_LIMIT_BYTES),
        name="mixer_prompt",
    )(x, npre, npost, wq, wr, sinks, lng, lnb, ws, bst, wap, wgp, wout)


def _mixer_sample_kernel(x_ref, ck_ref, cv_ref, npre_ref, npost_ref, wq_ref, wr_ref, sinks_ref,
                         lng_ref, lnb_ref, ws_ref, bst_ref, wap_ref, wgp_ref, wout_ref,
                         x1_ref, kout_ref, vout_ref, gv_ref,
                         a_s, bm_s):
    NB = ck_ref.shape[0]
    L = x_ref.shape[0] // NB
    keys = WINDOW + L
    x = x_ref[...]
    h = _rms(x, npre_ref[...]).astype(BF16)

    qb = (_dot(h, wq_ref[...]) * (HEAD_DIM ** -0.5 * LOG2E)).astype(BF16)
    kv = _dot(h, wr_ref[:, RO_K:RO_U])
    k = kv[:, 0:KV_W]
    v = kv[:, KV_W:2 * KV_W]
    rest = _dot(h, wr_ref[:, RO_U:REST_W])
    u = jax.nn.gelu(rest[:, 0:GMLP_W])
    gv = _layer_norm(jax.nn.gelu(rest[:, GMLP_W:2 * GMLP_W]), lng_ref[...], lnb_ref[...])
    gv_ref[...] = gv
    gvb = gv.astype(BF16)

    head_mask = _kv_head_masks(L)
    window_mask = _kv_head_masks(KEYS_PAD)
    zero_q = jnp.zeros((L, KV_W), BF16)
    zero_w = jnp.zeros((KEYS_PAD, KV_W), BF16)
    zero_rows = jnp.zeros((KEYS_PAD - keys, KV_W), BF16)
    valid = lax.broadcasted_iota(jnp.int32, (L, KEYS_PAD), 1) < keys
    causal = (lax.broadcasted_iota(jnp.int32, (L, L), 0)
              >= lax.broadcasted_iota(jnp.int32, (L, L), 1))
    wm = [jnp.where(causal, ws_ref[g], 0.0).astype(BF16) for g in range(GMLP_GROUPS)]

    def window(b, cache_ref, new):
        return jnp.concatenate([cache_ref[b].astype(BF16), new[b * L:(b + 1) * L, :].astype(BF16),
                                zero_rows], axis=0)

    def scores(b):
        lhs = []
        for g in range(Q_GROUP):
            qg = qb[b * L:(b + 1) * L, g * KV_W:(g + 1) * KV_W]
            lhs += [jnp.where(head_mask[j], qg, zero_q) for j in range(N_KV_HEADS)]
        return _dot_t(jnp.concatenate(lhs, axis=0), window(b, ck_ref, k))

    def softmax(s_all):
        rows = []
        for g in range(Q_GROUP):
            ps = []
            for j in range(N_KV_HEADS):
                i0 = (g * N_KV_HEADS + j) * L
                s = jnp.where(valid, s_all[i0:i0 + L, :], NEG_INF)
                sk = sinks_ref[j, g] * LOG2E
                m = jnp.maximum(jnp.max(s, axis=-1, keepdims=True), sk)
                p = jnp.exp2(s - m)
                denom = jnp.sum(p, axis=-1, keepdims=True) + jnp.exp2(sk - m)
                ps.append((p * (1.0 / denom)).astype(BF16))
            rows.append(jnp.concatenate(ps, axis=1))
        return jnp.concatenate(rows, axis=0)

    def pv(b, p_all):
        vw = window(b, cv_ref, v)
        bdv = jnp.concatenate([jnp.where(window_mask[j], vw, zero_w)
                               for j in range(N_KV_HEADS)], axis=0)
        o = _dot(p_all, bdv).astype(BF16)
        for g in range(Q_GROUP):
            a_s[b * L:(b + 1) * L, g * KV_W:(g + 1) * KV_W] = o[g * L:(g + 1) * L, :]

    s_next = scores(0)
    for b in range(NB):
        s_cur = s_next
        if b + 1 < NB:
            s_next = scores(b + 1)
        r0 = b * L
        kout_ref[b, 0:WINDOW - L, :] = ck_ref[b, L:WINDOW, :]
        kout_ref[b, WINDOW - L:WINDOW, :] = k[r0:r0 + L, :]
        vout_ref[b, 0:WINDOW - L, :] = cv_ref[b, L:WINDOW, :]
        vout_ref[b, WINDOW - L:WINDOW, :] = v[r0:r0 + L, :]
        for g in range(GMLP_GROUPS):
            c0 = g * GMLP_GW
            mix = _dot(wm[g], gvb[r0:r0 + L, c0:c0 + GMLP_GW]) + bst_ref[:, g:g + 1]
            bm_s[r0:r0 + L, c0:c0 + GMLP_GW] = (u[r0:r0 + L, c0:c0 + GMLP_GW] * mix).astype(BF16)
        pv(b, softmax(s_cur))

    merged = (jax.nn.sigmoid(rest[:, 2 * GMLP_W:2 * GMLP_W + D_MODEL]) * _dot(a_s[...], wap_ref[...])
              + jax.nn.sigmoid(rest[:, 2 * GMLP_W + D_MODEL:]) * _dot(bm_s[...], wgp_ref[...]))
    mo = _dot(merged.astype(BF16), wout_ref[...])
    x1_ref[...] = x + _rms(mo, npost_ref[...])


def _mixer_sample(x, ck, cv, npre, npost, wq, wr, sinks, lng, lnb, ws, bst, wap, wgp, wout):
    R = x.shape[0]
    NB = ck.shape[0]
    vmem = functools.partial(pl.BlockSpec, memory_space=pltpu.VMEM)
    return pl.pallas_call(
        _mixer_sample_kernel,
        in_specs=[vmem(), vmem(), vmem(), vmem(), vmem(), vmem(), vmem(),
                  pl.BlockSpec(memory_space=pltpu.SMEM),
                  vmem(), vmem(), vmem(), vmem(), vmem(), vmem(), vmem()],
        out_specs=[vmem(), vmem(), vmem(), vmem()],
        out_shape=[
            jax.ShapeDtypeStruct((R, D_MODEL), F32),
            jax.ShapeDtypeStruct((NB, WINDOW, KV_W), F32),
            jax.ShapeDtypeStruct((NB, WINDOW, KV_W), F32),
            jax.ShapeDtypeStruct((R, GMLP_W), F32),
        ],
        scratch_shapes=[
            pltpu.VMEM((R, ATTN_W), BF16),
            pltpu.VMEM((R, GMLP_W), BF16),
        ],
        compiler_params=pltpu.CompilerParams(vmem_limit_bytes=VMEM_LIMIT_BYTES),
        name="mixer_sample",
    )(x, ck, cv, npre, npost, wq, wr, sinks, lng, lnb, ws, bst, wap, wgp, wout)


def _ffn_kernel(x_ref, npre_ref, npost_ref, wfi_ref, wfo_ref, y_ref):
    sub = min(FFN_SUB, x_ref.shape[0])
    for base in range(0, x_ref.shape[0], sub):
        x = x_ref[base:base + sub, :]
        h = _rms(x, npre_ref[...]).astype(BF16)
        gate = _dot(h, wfi_ref[:, 0:D_FF])
        up = _dot(h, wfi_ref[:, D_FF:2 * D_FF])
        act = (jax.nn.silu(gate) * up).astype(BF16)
        y_ref[base:base + sub, :] = x + _rms(_dot(act, wfo_ref[...]), npost_ref[...])


def _ffn(x, npre, npost, wfi, wfo):
    R = x.shape[0]
    T = min(FFN_TILE, R)
    assert R % T == 0 and T % min(FFN_SUB, T) == 0
    row_spec = pl.BlockSpec((T, D_MODEL), lambda i: (i, 0))
    return pl.pallas_call(
        _ffn_kernel,
        grid=(R // T,),
        in_specs=[row_spec, _resident((1, D_MODEL)), _resident((1, D_MODEL)),
                  _resident((D_MODEL, 2 * D_FF)), _resident((D_FF, D_MODEL))],
        out_specs=row_spec,
        out_shape=jax.ShapeDtypeStruct((R, D_MODEL), F32),
        compiler_params=pltpu.CompilerParams(
            dimension_semantics=("arbitrary",),
            vmem_limit_bytes=VMEM_LIMIT_BYTES),
        name="ffn",
    )(x, npre, npost, wfi, wfo)


def _group_major(w, axis):
    shape = w.shape
    split = shape[:axis] + (N_KV_HEADS, Q_GROUP, HEAD_DIM) + shape[axis + 1:]
    return jnp.swapaxes(w.reshape(split), axis, axis + 1).reshape(shape)


def kernel(x_prompt, x_sample, cache_attn_k, cache_attn_v, norm_pre_mix, norm_post_mix, norm_pre_ffn,
           norm_post_ffn, w_in, attn_sinks, gmlp_ln_g, gmlp_ln_b, gmlp_w_s, gmlp_b_s, w_attn_proj,
           w_gmlp_proj, w_out, w_ffn_in, w_ffn_out):
    depth = w_in.shape[0]
    B, S, _ = x_prompt.shape
    NB, L, _ = x_sample.shape
    xp = x_prompt
    xs = x_sample.reshape(NB * L, D_MODEL)
    kp_l, vp_l, ks_l, vs_l, gs_l = [], [], [], [], []
    for l in range(depth):
        npre, npost = norm_pre_mix[l][None], norm_post_mix[l][None]
        fpre, fpost = norm_pre_ffn[l][None], norm_post_ffn[l][None]
        lng, lnb = gmlp_ln_g[l][None], gmlp_ln_b[l][None]
        wq = _group_major(w_in[l][:, :ATTN_W], 1).astype(BF16)
        wr = w_in[l][:, ATTN_W:].astype(BF16)
        wap = _group_major(w_attn_proj[l], 0).astype(BF16)
        wgp = w_gmlp_proj[l].astype(BF16)
        wout = w_out[l].astype(BF16)
        wfi = w_ffn_in[l].astype(BF16)
        wfo = w_ffn_out[l].astype(BF16)
        ws = gmlp_w_s[l]
        bst = gmlp_b_s[l].T

        x1p, kp, vp = _mixer_prompt(xp, npre, npost, wq, wr, attn_sinks[l], lng, lnb, ws, bst,
                                    wap, wgp, wout)
        xp = _ffn(x1p.reshape(B * S, D_MODEL), fpre, fpost, wfi, wfo).reshape(B, S, D_MODEL)

        ck = cache_attn_k[l].reshape(NB, WINDOW, KV_W)
        cv = cache_attn_v[l].reshape(NB, WINDOW, KV_W)
        x1s, ks, vs, gvs = _mixer_sample(xs, ck, cv, npre, npost, wq, wr, attn_sinks[l], lng, lnb,
                                         ws[:, :L, :L], bst[:L], wap, wgp, wout)
        xs = _ffn(x1s, fpre, fpost, wfi, wfo)

        kp_l.append(kp.reshape(B, WINDOW, N_KV_HEADS, HEAD_DIM))
        vp_l.append(vp.reshape(B, WINDOW, N_KV_HEADS, HEAD_DIM))
        ks_l.append(ks.reshape(NB, WINDOW, N_KV_HEADS, HEAD_DIM))
        vs_l.append(vs.reshape(NB, WINDOW, N_KV_HEADS, HEAD_DIM))
        gs_l.append(gvs.reshape(NB, L, GMLP_W))
    stack = (lambda xs_: xs_[0][None]) if depth == 1 else jnp.stack
    return (xp, xs.reshape(NB, L, D_MODEL), stack(kp_l), stack(vp_l), stack(ks_l), stack(vs_l),
            stack(gs_l))
```
